```python
import math
import jax, jax.numpy as jnp
from jax import lax
import numpy as np

D_MODEL = 1024
BATCH = 16
SEQ = 2048
DEPTH = 4
DEC_BATCH = 8
DEC_SEQ = 8192
PAST_LEN = 128

PLE_DIM = 256
N_HEADS = 8
HEAD_DIM = 64
V_DIM = 2 * HEAD_DIM
QK_WIDTH = N_HEADS * 2 * HEAD_DIM
ATTN_WIDTH = N_HEADS * V_DIM
CONV_WIDTH = D_MODEL
CONV_KERNEL = 31
D_FF = 2816
ROPE_THETA = 10000.0
Q_BLOCK = 128
LN_EPS = 1e-5
DEEPNORM_ALPHA = (2 * DEPTH) ** 0.25
DEEPNORM_BETA = (8 * DEPTH) ** -0.25
IN_SPLITS = (QK_WIDTH, 2 * QK_WIDTH, 2 * QK_WIDTH + ATTN_WIDTH,
             2 * QK_WIDTH + ATTN_WIDTH + 2 * CONV_WIDTH)
IN_WIDTH = 2 * QK_WIDTH + ATTN_WIDTH + 2 * CONV_WIDTH + 2 * D_MODEL

kernel_name = "hybrid_diffattn_conformer_encoder"


def layer_norm(x, g, b):
    xf = x.astype(jnp.float32)
    mu = jnp.mean(xf, axis=-1, keepdims=True)
    var = jnp.mean(jnp.square(xf - mu), axis=-1, keepdims=True)
    y = (xf - mu) * lax.rsqrt(var + LN_EPS)
    return (y * g.astype(jnp.float32) + b.astype(jnp.float32)).astype(x.dtype)


def rms_norm(x, g):
    xf = x.astype(jnp.float32)
    y = xf * lax.rsqrt(jnp.mean(jnp.square(xf), axis=-1, keepdims=True) + LN_EPS)
    return (y * g.astype(jnp.float32)).astype(x.dtype)


def swiglu(x, w_gate, w_up, w_down):
    return (jax.nn.silu(x @ w_gate) * (x @ w_up)) @ w_down


def rope(x, seq_len):
    half = HEAD_DIM // 2
    inv_freq = ROPE_THETA ** (-jnp.arange(half, dtype=jnp.float32) / half)
    ang = jnp.arange(seq_len, dtype=jnp.float32)[:, None] * inv_freq[None, :]
    cos = jnp.cos(ang)[None, :, None, None, :]
    sin = jnp.sin(ang)[None, :, None, None, :]
    xf = x.astype(jnp.float32)
    x1, x2 = xf[..., :half], xf[..., half:]
    return jnp.concatenate([x1 * cos - x2 * sin, x2 * cos + x1 * sin], axis=-1).astype(x.dtype)


def diff_attention(q, k, v, lam):
    B, S = q.shape[0], q.shape[1]
    nb = S // Q_BLOCK
    scale = HEAD_DIM ** -0.5
    qb = q.reshape(B, nb, Q_BLOCK, N_HEADS, 2, HEAD_DIM).transpose(1, 0, 2, 3, 4, 5)

    def block(q_blk):
        s = jnp.einsum('bqhcd,bkhcd->bhcqk', q_blk, k).astype(jnp.float32) * scale
        probs = jax.nn.softmax(s, axis=-1)
        a = probs[:, :, 0] - lam * probs[:, :, 1]
        return jnp.einsum('bhqk,bkhe->bqhe', a.astype(v.dtype), v)

    out = lax.map(block, qb)
    return out.transpose(1, 0, 2, 3, 4).reshape(B, S, N_HEADS, V_DIM)


def depthwise_conv(x, w, b):
    pad = CONV_KERNEL // 2
    y = lax.conv_general_dilated(x, w[:, None, :].astype(x.dtype), window_strides=(1,),
                                 padding=[(pad, pad)],
                                 dimension_numbers=('NWC', 'WIO', 'NWC'),
                                 feature_group_count=CONV_WIDTH)
    return y + b


def token_mixing(x, layer_idx, w_in, lam_q1, lam_k1, lam_q2, lam_k2, subln_g,
                 conv_dw, conv_dw_b, conv_ln_g, conv_ln_b, conv_pw2, w_o):
    B, S, _ = x.shape
    z = x @ w_in
    q, k, v, u, gates = jnp.split(z, IN_SPLITS, axis=-1)
    q = rope(q.reshape(B, S, N_HEADS, 2, HEAD_DIM), S)
    k = rope(k.reshape(B, S, N_HEADS, 2, HEAD_DIM), S)
    v = v.reshape(B, S, N_HEADS, V_DIM)
    lam_init = 0.8 - 0.6 * math.exp(-0.3 * layer_idx)
    lam = (jnp.exp(jnp.sum(lam_q1.astype(jnp.float32) * lam_k1.astype(jnp.float32)))
           - jnp.exp(jnp.sum(lam_q2.astype(jnp.float32) * lam_k2.astype(jnp.float32)))
           + lam_init)
    o = diff_attention(q, k, v, lam)
    o = rms_norm(o, subln_g) * (1.0 - lam_init)
    attn_out = o.reshape(B, S, ATTN_WIDTH)
    c = u[..., :CONV_WIDTH] * jax.nn.sigmoid(u[..., CONV_WIDTH:])
    c = depthwise_conv(c, conv_dw, conv_dw_b)
    c = jax.nn.silu(layer_norm(c, conv_ln_g, conv_ln_b))
    conv_out = c @ conv_pw2
    g = jax.nn.sigmoid(gates)
    merged = g[..., :D_MODEL] * attn_out + g[..., D_MODEL:] * conv_out
    return merged @ w_o


def trunk(x, p, ffn1_w_gate, ffn1_w_up, ffn1_w_down, ln1_g, ln1_b,
          w_in, lam_q1, lam_k1, lam_q2, lam_k2, subln_g,
          conv_dw, conv_dw_b, conv_ln_g, conv_ln_b, conv_pw2, w_o, ln2_g, ln2_b,
          ffn2_w_gate, ffn2_w_up, ffn2_w_down, ple_w_gate, ple_w_proj, ln3_g, ln3_b):
    for i in range(DEPTH):
        x = layer_norm(DEEPNORM_ALPHA * x + 0.5 * swiglu(x, ffn1_w_gate[i], ffn1_w_up[i], ffn1_w_down[i]),
                       ln1_g[i], ln1_b[i])
        mix = token_mixing(x, i, w_in[i], lam_q1[i], lam_k1[i], lam_q2[i], lam_k2[i], subln_g[i],
                           conv_dw[i], conv_dw_b[i], conv_ln_g[i], conv_ln_b[i], conv_pw2[i], w_o[i])
        x = layer_norm(DEEPNORM_ALPHA * x + mix, ln2_g[i], ln2_b[i])
        ple = jax.nn.sigmoid(x @ ple_w_gate[i]) * (p[i] @ ple_w_proj[i])
        x = layer_norm(DEEPNORM_ALPHA * x + 0.5 * swiglu(x, ffn2_w_gate[i], ffn2_w_up[i], ffn2_w_down[i]) + ple,
                       ln3_g[i], ln3_b[i])
    return x


def setup_inputs(seed: int = 0) -> dict:
    key = jax.random.key(seed)
    ks = jax.random.split(key, 32)
    f32 = jnp.float32

    def nrm(k, shape, scale):
        return jax.random.normal(k, shape, dtype=f32) * scale

    def gain(k, shape):
        return 1.0 + 0.02 * jax.random.normal(k, shape, dtype=f32)

    L, D, F, C = DEPTH, D_MODEL, D_FF, CONV_WIDTH
    return {
        "x_prompt": nrm(ks[0], (BATCH, SEQ, D), 1.0),
        "x_sample": nrm(ks[1], (DEC_BATCH, DEC_SEQ, D), 1.0),
        "p_prompt": nrm(ks[2], (DEPTH, BATCH, SEQ, PLE_DIM), 1.0),
        "p_sample": nrm(ks[3], (DEPTH, DEC_BATCH, DEC_SEQ, PLE_DIM), 1.0),
        "ffn1_w_gate": nrm(ks[4], (L, D, F), D ** -0.5),
        "ffn1_w_up": nrm(ks[5], (L, D, F), D ** -0.5),
        "ffn1_w_down": nrm(ks[6], (L, F, D), F ** -0.5 * DEEPNORM_BETA),
        "ln1_g": gain(ks[7], (L, D)),
        "ln1_b": nrm(ks[8], (L, D), 0.02),
        "w_in": nrm(ks[9], (L, D, IN_WIDTH), D ** -0.5),
        "lam_q1": nrm(ks[10], (L, HEAD_DIM), 0.1),
        "lam_k1": nrm(ks[11], (L, HEAD_DIM), 0.1),
        "lam_q2": nrm(ks[12], (L, HEAD_DIM), 0.1),
        "lam_k2": nrm(ks[13], (L, HEAD_DIM), 0.1),
        "subln_g": gain(ks[14], (L, V_DIM)),
        "conv_dw": nrm(ks[15], (L, CONV_KERNEL, C), CONV_KERNEL ** -0.5),
        "conv_dw_b": nrm(ks[16], (L, C), 0.02),
        "conv_ln_g": gain(ks[17], (L, C)),
        "conv_ln_b": nrm(ks[18], (L, C), 0.02),
        "conv_pw2": nrm(ks[19], (L, C, D), C ** -0.5),
        "w_o": nrm(ks[20], (L, D, D), D ** -0.5 * DEEPNORM_BETA),
        "ln2_g": gain(ks[21], (L, D)),
        "ln2_b": nrm(ks[22], (L, D), 0.02),
        "ffn2_w_gate": nrm(ks[23], (L, D, F), D ** -0.5),
        "ffn2_w_up": nrm(ks[24], (L, D, F), D ** -0.5),
        "ffn2_w_down": nrm(ks[25], (L, F, D), F ** -0.5 * DEEPNORM_BETA),
        "ple_w_gate": nrm(ks[26], (L, D, D), D ** -0.5),
        "ple_w_proj": nrm(ks[27], (L, PLE_DIM, D), PLE_DIM ** -0.5 * DEEPNORM_BETA),
        "ln3_g": gain(ks[28], (L, D)),
        "ln3_b": nrm(ks[29], (L, D), 0.02),
    }


def reference(x_prompt, x_sample, p_prompt, p_sample,
              ffn1_w_gate, ffn1_w_up, ffn1_w_down, ln1_g, ln1_b,
              w_in, lam_q1, lam_k1, lam_q2, lam_k2, subln_g,
              conv_dw, conv_dw_b, conv_ln_g, conv_ln_b, conv_pw2, w_o, ln2_g, ln2_b,
              ffn2_w_gate, ffn2_w_up, ffn2_w_down, ple_w_gate, ple_w_proj, ln3_g, ln3_b):
    y_prompt = trunk(x_prompt, p_prompt, ffn1_w_gate, ffn1_w_up, ffn1_w_down, ln1_g, ln1_b,
                     w_in, lam_q1, lam_k1, lam_q2, lam_k2, subln_g,
                     conv_dw, conv_dw_b, conv_ln_g, conv_ln_b, conv_pw2, w_o, ln2_g, ln2_b,
                     ffn2_w_gate, ffn2_w_up, ffn2_w_down, ple_w_gate, ple_w_proj, ln3_g, ln3_b)
    y_sample = trunk(x_sample, p_sample, ffn1_w_gate, ffn1_w_up, ffn1_w_down, ln1_g, ln1_b,
                     w_in, lam_q1, lam_k1, lam_q2, lam_k2, subln_g,
                     conv_dw, conv_dw_b, conv_ln_g, conv_ln_b, conv_pw2, w_o, ln2_g, ln2_b,
                     ffn2_w_gate, ffn2_w_up, ffn2_w_down, ple_w_gate, ple_w_proj, ln3_g, ln3_b)
    return (y_prompt, y_sample)
```

```python
import functools
import math

import jax
import jax.numpy as jnp
from jax.experimental import pallas as pl
from jax.experimental.pallas import tpu as pltpu

D_MODEL = 1024
DEPTH = 4
PLE_DIM = 256
N_HEADS = 8
HEAD_DIM = 64
V_DIM = 2 * HEAD_DIM
CONV_KERNEL = 31
CONV_PAD = CONV_KERNEL // 2
D_FF = 2816
ROPE_THETA = 10000.0
LN_EPS = 1e-5
ALPHA = (2 * DEPTH) ** 0.25

LANES = 128
HALO = 16
VMEM_LIMIT = 56 * 1024 * 1024

BF16 = jnp.bfloat16
F32 = jnp.float32


def _dot(a, b):
    return jnp.dot(a, b, preferred_element_type=F32)


def _dot_nt(a, b):
    return jax.lax.dot_general(a, b, (((1,), (1,)), ((), ())), preferred_element_type=F32)


def _layer_norm(y, g, b):
    mu = jnp.mean(y, axis=-1, keepdims=True)
    d = y - mu
    var = jnp.mean(d * d, axis=-1, keepdims=True)
    return d * jax.lax.rsqrt(var + LN_EPS) * g + b


def _const_spec(shape):
    return pl.BlockSpec(shape, lambda *_: (0,) * len(shape), pipeline_mode=pl.Buffered(1))


FF_CHUNK = 256


def _ffn_kernel(*refs, with_ple):
    if with_ple:
        (x_ref, p_ref, wg_ref, wu_ref, wd_ref, pg_ref, pp_ref, g_ref, b_ref, o_ref) = refs
    else:
        (x_ref, wg_ref, wu_ref, wd_ref, g_ref, b_ref, o_ref) = refs
    x = x_ref[...]
    xb = x.astype(BF16)
    acc = jnp.zeros(x.shape, F32)
    for f in range(0, D_FF, FF_CHUNK):
        hg = _dot(xb, wg_ref[:, f:f + FF_CHUNK])
        hu = _dot(xb, wu_ref[:, f:f + FF_CHUNK])
        h = (hg * jax.nn.sigmoid(hg)) * hu
        acc = acc + _dot(h.astype(BF16), wd_ref[f:f + FF_CHUNK, :])
    y = ALPHA * x + 0.5 * acc
    if with_ple:
        gate = jax.nn.sigmoid(_dot(xb, pg_ref[...]))
        y = y + gate * _dot(p_ref[...].astype(BF16), pp_ref[...])
    o_ref[...] = _layer_norm(y, g_ref[...], b_ref[...])


def _ffn(x, wg, wu, wd, g, b, ple=None, *, tm=512):
    t, d = x.shape
    row = lambda w: pl.BlockSpec((tm, w), lambda i: (i, 0))
    ins = [x]
    specs = [row(d)]
    if ple is not None:
        p, pg, pp = ple
        ins.append(p)
        specs.append(row(PLE_DIM))
    ins += [wg, wu, wd]
    specs += [_const_spec(wg.shape), _const_spec(wu.shape), _const_spec(wd.shape)]
    if ple is not None:
        ins += [pg, pp]
        specs += [_const_spec(pg.shape), _const_spec(pp.shape)]
    ins += [g, b]
    specs += [_const_spec(g.shape), _const_spec(b.shape)]
    return pl.pallas_call(
        functools.partial(_ffn_kernel, with_ple=ple is not None),
        out_shape=jax.ShapeDtypeStruct((t, d), F32),
        grid=(t // tm,),
        in_specs=specs,
        out_specs=row(d),
        compiler_params=pltpu.CompilerParams(
            dimension_semantics=("arbitrary",), vmem_limit_bytes=VMEM_LIMIT),
        name="ffn_ple" if ple is not None else "ffn",
    )(*ins)


QK_SCALE = HEAD_DIM ** -0.5 * math.log2(math.e)


def _rope_tables(seq_len):
    half = HEAD_DIM // 2
    inv_freq = ROPE_THETA ** (-jnp.arange(half, dtype=F32) / half)
    ang = jnp.arange(seq_len, dtype=F32)[:, None] * inv_freq[None, :]
    cos, sin = jnp.cos(ang), jnp.sin(ang)
    cos = jnp.concatenate([cos, cos], axis=-1)
    sin = jnp.concatenate([-sin, sin], axis=-1)
    reps = LANES // HEAD_DIM
    return jnp.tile(cos, (1, reps)), jnp.tile(sin, (1, reps))


def _rope(z, cos, sin, first_half):
    half = HEAD_DIM // 2
    fwd = pltpu.roll(z, LANES - half, axis=1)
    bwd = pltpu.roll(z, half, axis=1)
    return z * cos + jnp.where(first_half, fwd, bwd) * sin


def _inproj_kernel(x_ref, w_ref, cos_ref, sin_ref, q_ref, k_ref, v_ref, c_ref, g_ref):
    xb = x_ref[...].astype(BF16)
    cos = cos_ref[...]
    sin = sin_ref[...]
    lane = jax.lax.broadcasted_iota(jnp.int32, cos.shape, 1)
    first_half = (lane % HEAD_DIM) < (HEAD_DIM // 2)
    d = D_MODEL
    for j in range(d // LANES):
        zq = _dot(xb, w_ref[:, j * LANES:(j + 1) * LANES])
        q_ref[:, j * LANES:(j + 1) * LANES] = (_rope(zq, cos, sin, first_half) * QK_SCALE).astype(BF16)
        zk = _dot(xb, w_ref[:, d + j * LANES:d + (j + 1) * LANES])
        k_ref[:, j * LANES:(j + 1) * LANES] = _rope(zk, cos, sin, first_half).astype(BF16)
    v_ref[...] = _dot(xb, w_ref[:, 2 * d:3 * d]).astype(BF16)
    ua = _dot(xb, w_ref[:, 3 * d:4 * d])
    ub = _dot(xb, w_ref[:, 4 * d:5 * d])
    c_ref[...] = ua * jax.nn.sigmoid(ub)
    g_ref[...] = jax.nn.sigmoid(_dot(xb, w_ref[:, 5 * d:7 * d]))


def _inproj(x, w_in, cos, sin, seq_len, *, tm=512):
    t, d = x.shape
    per_seq = seq_len // tm
    row = lambda w: pl.BlockSpec((tm, w), lambda i: (i, 0))
    pos = pl.BlockSpec((tm, LANES), lambda i: (i % per_seq, 0))
    return pl.pallas_call(
        _inproj_kernel,
        out_shape=(jax.ShapeDtypeStruct((t, d), BF16), jax.ShapeDtypeStruct((t, d), BF16),
                   jax.ShapeDtypeStruct((t, d), BF16), jax.ShapeDtypeStruct((t, d), F32),
                   jax.ShapeDtypeStruct((t, 2 * d), F32)),
        grid=(t // tm,),
        in_specs=[row(d), _const_spec(w_in.shape), pos, pos],
        out_specs=(row(d), row(d), row(d), row(d), row(2 * d)),
        compiler_params=pltpu.CompilerParams(
            dimension_semantics=("arbitrary",), vmem_limit_bytes=VMEM_LIMIT),
        name="inproj",
    )(x, w_in, cos, sin)


def _attn_kernel(q_ref, k_ref, v_ref, lq1_ref, lk1_ref, lq2_ref, lk2_ref, sg_ref, o_ref,
                 acc0_ref, acc1_ref, *, tkv, lam_init):
    q = q_ref[0]
    tq = q.shape[0]
    n_chunks = k_ref.shape[1] // tkv
    lane = jax.lax.broadcasted_iota(jnp.int32, (tkv, LANES), 1)
    map0 = lane < HEAD_DIM

    acc0_ref[...] = jnp.zeros(acc0_ref.shape, F32)
    acc1_ref[...] = jnp.zeros(acc1_ref.shape, F32)

    def step(c, carry):
        m0, l0, m1, l1 = carry
        start = pl.multiple_of(c * tkv, tkv)
        kc = k_ref[0, pl.ds(start, tkv), :]
        vc = v_ref[0, pl.ds(start, tkv), :]
        zero = jnp.zeros_like(kc)
        s0 = _dot_nt(q, jnp.where(map0, kc, zero))
        s1 = _dot_nt(q, jnp.where(map0, zero, kc))

        def online(s, m, l, acc_ref):
            m_new = jnp.maximum(m, jnp.max(s, axis=-1, keepdims=True))
            scale = jnp.exp2(m - m_new)
            p = jnp.exp2(s - m_new)
            l_new = scale * l + jnp.sum(p, axis=-1, keepdims=True)
            acc_ref[...] = scale * acc_ref[...] + _dot(p.astype(BF16), vc)
            return m_new, l_new

        m0, l0 = online(s0, m0, l0, acc0_ref)
        m1, l1 = online(s1, m1, l1, acc1_ref)
        return m0, l0, m1, l1

    neg = jnp.full((tq, 1), -jnp.inf, F32)
    zer = jnp.zeros((tq, 1), F32)
    _, l0, _, l1 = jax.lax.fori_loop(0, n_chunks, step, (neg, zer, neg, zer))

    lam = (jnp.exp(jnp.sum(lq1_ref[...] * lk1_ref[...], axis=-1, keepdims=True))
           - jnp.exp(jnp.sum(lq2_ref[...] * lk2_ref[...], axis=-1, keepdims=True)) + lam_init)
    o = acc0_ref[...] / l0 - lam * (acc1_ref[...] / l1)
    ms = jnp.mean(o * o, axis=-1, keepdims=True)
    o_ref[0] = o * jax.lax.rsqrt(ms + LN_EPS) * sg_ref[...] * (1.0 - lam_init)


def _attention(q, k, v, lq1, lk1, lq2, lk2, subln_g, lam_init, *, tq=256, tkv=512):
    b, s, d = q.shape
    tq = min(tq, s)
    tkv = min(tkv, s)
    qo = pl.BlockSpec((1, tq, V_DIM), lambda bi, h, qi: (bi, qi, h))
    kv = pl.BlockSpec((1, s, V_DIM), lambda bi, h, qi: (bi, 0, h))
    return pl.pallas_call(
        functools.partial(_attn_kernel, tkv=tkv, lam_init=lam_init),
        out_shape=jax.ShapeDtypeStruct((b, s, d), F32),
        grid=(b, N_HEADS, s // tq),
        in_specs=[qo, kv, kv] + [_const_spec(lq1.shape)] * 4 + [_const_spec(subln_g.shape)],
        out_specs=qo,
        scratch_shapes=[pltpu.VMEM((tq, V_DIM), F32), pltpu.VMEM((tq, V_DIM), F32)],
        compiler_params=pltpu.CompilerParams(
            dimension_semantics=("arbitrary", "arbitrary", "arbitrary"), vmem_limit_bytes=VMEM_LIMIT),
        name="diff_attn",
    )(q, k, v, lq1, lk1, lq2, lk2, subln_g)


def _post_kernel(c_ref, cp_ref, cn_ref, a_ref, g_ref, x_ref, dw_ref, dwb_ref, cg_ref, cb_ref,
                 pw_ref, wo_ref, g2_ref, b2_ref, o_ref, ext_ref):
    ts = c_ref.shape[1]
    si = pl.program_id(1)
    n_s = pl.num_programs(1)
    n_lt = D_MODEL // LANES
    prev_ok = (si > 0).astype(F32)
    next_ok = (si < n_s - 1).astype(F32)
    for j in range(n_lt):
        ls = slice(j * LANES, (j + 1) * LANES)
        ext_ref[j, 0:HALO, :] = cp_ref[0, :, ls] * prev_ok
        ext_ref[j, HALO:HALO + ts, :] = c_ref[0, :, ls]
        ext_ref[j, HALO + ts:, :] = cn_ref[0, :, ls] * next_ok
    pieces = []
    for j in range(n_lt):
        ls = slice(j * LANES, (j + 1) * LANES)
        acc = jnp.broadcast_to(dwb_ref[:, ls], (ts, LANES))
        for t in range(CONV_KERNEL):
            off = HALO - CONV_PAD + t
            acc = acc + dw_ref[t:t + 1, ls] * ext_ref[j, off:off + ts, :]
        pieces.append(acc)
    y = jnp.concatenate(pieces, axis=-1)
    y = _layer_norm(y, cg_ref[...], cb_ref[...])
    y = y * jax.nn.sigmoid(y)
    conv_out = _dot(y.astype(BF16), pw_ref[...])
    merged = g_ref[0, :, :D_MODEL] * a_ref[0] + g_ref[0, :, D_MODEL:] * conv_out
    mix = _dot(merged.astype(BF16), wo_ref[...])
    o_ref[0] = _layer_norm(ALPHA * x_ref[0] + mix, g2_ref[...], b2_ref[...])


def _post(c, attn, gates, x, dw, dwb, cg, cb, pw, wo, g2, b2, *, ts=256):
    b, s, d = c.shape
    ts = min(ts, s)
    hb = ts // HALO
    n_hb = s // HALO
    main = lambda w: pl.BlockSpec((1, ts, w), lambda bi, si: (bi, si, 0))
    prev = pl.BlockSpec((1, HALO, d), lambda bi, si: (bi, jnp.maximum(si * hb - 1, 0), 0))
    nxt = pl.BlockSpec((1, HALO, d), lambda bi, si: (bi, jnp.minimum((si + 1) * hb, n_hb - 1), 0))
    consts = [dw, dwb, cg, cb, pw, wo, g2, b2]
    return pl.pallas_call(
        _post_kernel,
        out_shape=jax.ShapeDtypeStruct((b, s, d), F32),
        grid=(b, s // ts),
        in_specs=[main(d), prev, nxt, main(d), main(2 * d), main(d)] + [_const_spec(w.shape) for w in consts],
        out_specs=main(d),
        scratch_shapes=[pltpu.VMEM((d // LANES, ts + 2 * HALO, LANES), F32)],
        compiler_params=pltpu.CompilerParams(
            dimension_semantics=("arbitrary", "arbitrary"), vmem_limit_bytes=VMEM_LIMIT),
        name="conv_merge_out",
    )(c, c, c, attn, gates, x, *consts)


def _trunk(x, p, w):
    b, s, d = x.shape
    t = b * s
    cos, sin = _rope_tables(s)
    x = x.reshape(t, d)
    row = lambda a: a.reshape(1, -1)
    for i in range(DEPTH):
        lam_init = 0.8 - 0.6 * math.exp(-0.3 * i)
        x = _ffn(x, w["ffn1_w_gate"][i], w["ffn1_w_up"][i], w["ffn1_w_down"][i],
                 row(w["ln1_g"][i]), row(w["ln1_b"][i]))
        q, k, v, c, g = _inproj(x, w["w_in"][i], cos, sin, s)
        shp = lambda a: a.reshape(b, s, -1)
        attn = _attention(shp(q), shp(k), shp(v), row(w["lam_q1"][i]), row(w["lam_k1"][i]),
                          row(w["lam_q2"][i]), row(w["lam_k2"][i]), row(w["subln_g"][i]), lam_init)
        x = _post(shp(c), attn, shp(g), shp(x), w["conv_dw"][i], row(w["conv_dw_b"][i]),
                  row(w["conv_ln_g"][i]), row(w["conv_ln_b"][i]), w["conv_pw2"][i], w["w_o"][i],
                  row(w["ln2_g"][i]), row(w["ln2_b"][i])).reshape(t, d)
        x = _ffn(x, w["ffn2_w_gate"][i], w["ffn2_w_up"][i], w["ffn2_w_down"][i],
                 row(w["ln3_g"][i]), row(w["ln3_b"][i]),
                 ple=(p[i].reshape(t, PLE_DIM), w["ple_w_gate"][i], w["ple_w_proj"][i]))
    return x.reshape(b, s, d)


_MATMUL_WEIGHTS = ("ffn1_w_gate", "ffn1_w_up", "ffn1_w_down", "w_in", "conv_pw2", "w_o",
                   "ffn2_w_gate", "ffn2_w_up", "ffn2_w_down", "ple_w_gate", "ple_w_proj")


def kernel(x_prompt, x_sample, p_prompt, p_sample, ffn1_w_gate, ffn1_w_up, ffn1_w_down, ln1_g, ln1_b, w_in, lam_q1, lam_k1, lam_q2, lam_k2, subln_g, conv_dw, conv_dw_b, conv_ln_g, conv_ln_b, conv_pw2, w_o, ln2_g, ln2_b, ffn2_w_gate, ffn2_w_up, ffn2_w_down, ple_w_gate, ple_w_proj, ln3_g, ln3_b):
    w = dict(ffn1_w_gate=ffn1_w_gate, ffn1_w_up=ffn1_w_up, ffn1_w_down=ffn1_w_down, ln1_g=ln1_g,
             ln1_b=ln1_b, w_in=w_in, lam_q1=lam_q1, lam_k1=lam_k1, lam_q2=lam_q2, lam_k2=lam_k2,
             subln_g=subln_g, conv_dw=conv_dw, conv_dw_b=conv_dw_b, conv_ln_g=conv_ln_g,
             conv_ln_b=conv_ln_b, conv_pw2=conv_pw2, w_o=w_o, ln2_g=ln2_g, ln2_b=ln2_b,
             ffn2_w_gate=ffn2_w_gate, ffn2_w_up=ffn2_w_up, ffn2_w_down=ffn2_w_down,
             ple_w_gate=ple_w_gate, ple_w_proj=ple_w_proj, ln3_g=ln3_g, ln3_b=ln3_b)
    for name in _MATMUL_WEIGHTS:
        w[name] = w[name].astype(BF16)
    return _trunk(x_prompt, p_prompt, w), _trunk(x_sample, p_sample, w)
```

```python
import functools
import math

import jax
import jax.numpy as jnp
from jax.experimental import pallas as pl
from jax.experimental.pallas import tpu as pltpu

D_MODEL = 1024
DEPTH = 4
PLE_DIM = 256
N_HEADS = 8
HEAD_DIM = 64
V_DIM = 2 * HEAD_DIM
CONV_KERNEL = 31
CONV_PAD = CONV_KERNEL // 2
D_FF = 2816
ROPE_THETA = 10000.0
LN_EPS = 1e-5
ALPHA = (2 * DEPTH) ** 0.25

LANES = 128
HALO = 16
VMEM_LIMIT = 56 * 1024 * 1024

BF16 = jnp.bfloat16
F32 = jnp.float32


def _dot(a, b):
    return jnp.dot(a, b, preferred_element_type=F32)


def _dot_nt(a, b):
    return jax.lax.dot_general(a, b, (((1,), (1,)), ((), ())), preferred_element_type=F32)


def _layer_norm(y, g, b):
    mu = jnp.mean(y, axis=-1, keepdims=True)
    d = y - mu
    var = jnp.mean(d * d, axis=-1, keepdims=True)
    return d * jax.lax.rsqrt(var + LN_EPS) * g + b


def _const_spec(shape):
    return pl.BlockSpec(shape, lambda *_: (0,) * len(shape), pipeline_mode=pl.Buffered(1))


FF_CHUNK = 256


def _ffn_kernel(*refs, with_ple):
    if with_ple:
        (x_ref, p_ref, wg_ref, wu_ref, wd_ref, pg_ref, pp_ref, g_ref, b_ref, o_ref) = refs
    else:
        (x_ref, wg_ref, wu_ref, wd_ref, g_ref, b_ref, o_ref) = refs
    x = x_ref[...]
    xb = x.astype(BF16)
    acc = jnp.zeros(x.shape, F32)
    for f in range(0, D_FF, FF_CHUNK):
        hg = _dot(xb, wg_ref[:, f:f + FF_CHUNK])
        hu = _dot(xb, wu_ref[:, f:f + FF_CHUNK])
        h = (hg * jax.nn.sigmoid(hg)) * hu
        acc = acc + _dot(h.astype(BF16), wd_ref[f:f + FF_CHUNK, :])
    y = ALPHA * x + 0.5 * acc
    if with_ple:
        gate = jax.nn.sigmoid(_dot(xb, pg_ref[...]))
        y = y + gate * _dot(p_ref[...].astype(BF16), pp_ref[...])
    o_ref[...] = _layer_norm(y, g_ref[...], b_ref[...])


def _ffn(x, wg, wu, wd, g, b, ple=None, *, tm=512):
    t, d = x.shape
    row = lambda w: pl.BlockSpec((tm, w), lambda i: (i, 0))
    ins = [x]
    specs = [row(d)]
    if ple is not None:
        p, pg, pp = ple
        ins.append(p)
        specs.append(row(PLE_DIM))
    ins += [wg, wu, wd]
    specs += [_const_spec(wg.shape), _const_spec(wu.shape), _const_spec(wd.shape)]
    if ple is not None:
        ins += [pg, pp]
        specs += [_const_spec(pg.shape), _const_spec(pp.shape)]
    ins += [g, b]
    specs += [_const_spec(g.shape), _const_spec(b.shape)]
    return pl.pallas_call(
        functools.partial(_ffn_kernel, with_ple=ple is not None),
        out_shape=jax.ShapeDtypeStruct((t, d), F32),
        grid=(t // tm,),
        in_specs=specs,
        out_specs=row(d),
        compiler_params=pltpu.CompilerParams(
            dimension_semantics=("arbitrary",), vmem_limit_bytes=VMEM_LIMIT),
        name="ffn_ple" if ple is not None else "ffn",
    )(*ins)


QK_SCALE = HEAD_DIM ** -0.5 * math.log2(math.e)


def _rope_tables(seq_len):
    half = HEAD_DIM // 2
    inv_freq = ROPE_THETA ** (-jnp.arange(half, dtype=F32) / half)
    ang = jnp.arange(seq_len, dtype=F32)[:, None] * inv_freq[None, :]
    cos, sin = jnp.cos(ang), jnp.sin(ang)
    cos = jnp.concatenate([cos, cos], axis=-1)
    sin = jnp.concatenate([-sin, sin], axis=-1)
    reps = LANES // HEAD_DIM
    return jnp.tile(cos, (1, reps)), jnp.tile(sin, (1, reps))


def _rope(z, cos, sin, first_half):
    half = HEAD_DIM // 2
    fwd = pltpu.roll(z, LANES - half, axis=1)
    bwd = pltpu.roll(z, half, axis=1)
    return z * cos + jnp.where(first_half, fwd, bwd) * sin


def _inproj_kernel(x_ref, w_ref, cos_ref, sin_ref, q_ref, k_ref, v_ref, c_ref, g_ref):
    xb = x_ref[...].astype(BF16)
    cos = cos_ref[...]
    sin = sin_ref[...]
    lane = jax.lax.broadcasted_iota(jnp.int32, cos.shape, 1)
    first_half = (lane % HEAD_DIM) < (HEAD_DIM // 2)
    d = D_MODEL
    for j in range(d // LANES):
        zq = _dot(xb, w_ref[:, j * LANES:(j + 1) * LANES])
        q_ref[:, j * LANES:(j + 1) * LANES] = (_rope(zq, cos, sin, first_half) * QK_SCALE).astype(BF16)
        zk = _dot(xb, w_ref[:, d + j * LANES:d + (j + 1) * LANES])
        k_ref[:, j * LANES:(j + 1) * LANES] = _rope(zk, cos, sin, first_half).astype(BF16)
    v_ref[...] = _dot(xb, w_ref[:, 2 * d:3 * d]).astype(BF16)
    ua = _dot(xb, w_ref[:, 3 * d:4 * d])
    ub = _dot(xb, w_ref[:, 4 * d:5 * d])
    c_ref[...] = ua * jax.nn.sigmoid(ub)
    g_ref[...] = jax.nn.sigmoid(_dot(xb, w_ref[:, 5 * d:7 * d]))


def _inproj(x, w_in, cos, sin, seq_len, *, tm=512):
    t, d = x.shape
    per_seq = seq_len // tm
    row = lambda w: pl.BlockSpec((tm, w), lambda i: (i, 0))
    pos = pl.BlockSpec((tm, LANES), lambda i: (i % per_seq, 0))
    return pl.pallas_call(
        _inproj_kernel,
        out_shape=(jax.ShapeDtypeStruct((t, d), BF16), jax.ShapeDtypeStruct((t, d), BF16),
                   jax.ShapeDtypeStruct((t, d), BF16), jax.ShapeDtypeStruct((t, d), F32),
                   jax.ShapeDtypeStruct((t, 2 * d), F32)),
        grid=(t // tm,),
        in_specs=[row(d), _const_spec(w_in.shape), pos, pos],
        out_specs=(row(d), row(d), row(d), row(d), row(2 * d)),
        compiler_params=pltpu.CompilerParams(
            dimension_semantics=("arbitrary",), vmem_limit_bytes=VMEM_LIMIT),
        name="inproj",
    )(x, w_in, cos, sin)


CHUNK_UNROLL = 8


def _attn_kernel(q_ref, k_ref, v_ref, lq1_ref, lk1_ref, lq2_ref, lk2_ref, sg_ref, o_ref,
                 s_ref, m_ref, l_ref, acc_ref, *, tq, tkv, lam_init):
    seq = k_ref.shape[1]
    n_q = seq // tq
    n_c = seq // tkv
    n_lt = tkv // LANES
    map0 = jax.lax.broadcasted_iota(jnp.int32, (tq, LANES), 1) < HEAD_DIM
    lam = (jnp.exp(jnp.sum(lq1_ref[...] * lk1_ref[...], axis=-1, keepdims=True))
           - jnp.exp(jnp.sum(lq2_ref[...] * lk2_ref[...], axis=-1, keepdims=True)) + lam_init)
    out_gain = sg_ref[...] * (1.0 - lam_init)

    def q_tile(qi, carry):
        q = q_ref[0, pl.ds(pl.multiple_of(qi * tq, tq), tq), :]
        zero = jnp.zeros_like(q)
        qs = jnp.concatenate([jnp.where(map0, q, zero), jnp.where(map0, zero, q)], axis=0)
        m_ref[...] = jnp.full(m_ref.shape, -jnp.inf, F32)
        l_ref[...] = jnp.zeros(l_ref.shape, F32)
        acc_ref[...] = jnp.zeros(acc_ref.shape, F32)

        def scores(c, carry):
            kc = k_ref[0, pl.ds(pl.multiple_of(c * tkv, tkv), tkv), :]
            s = _dot_nt(qs, kc)
            s_ref[c] = s
            m = m_ref[...]
            for j in range(n_lt):
                m = jnp.maximum(m, s[:, j * LANES:(j + 1) * LANES])
            m_ref[...] = m
            return carry

        jax.lax.fori_loop(0, n_c, scores, 0, unroll=min(CHUNK_UNROLL, n_c))
        m_ref[...] = jnp.broadcast_to(jnp.max(m_ref[...], axis=-1, keepdims=True), m_ref.shape)

        def weighted(c, carry):
            vc = v_ref[0, pl.ds(pl.multiple_of(c * tkv, tkv), tkv), :]
            m = m_ref[...]
            l = l_ref[...]
            ps = []
            for j in range(n_lt):
                p = jnp.exp2(s_ref[c, :, j * LANES:(j + 1) * LANES] - m)
                l = l + p
                ps.append(p.astype(BF16))
            l_ref[...] = l
            acc_ref[...] += _dot(jnp.concatenate(ps, axis=-1), vc)
            return carry

        jax.lax.fori_loop(0, n_c, weighted, 0, unroll=min(CHUNK_UNROLL, n_c))
        inv_l = 1.0 / jnp.sum(l_ref[...], axis=-1, keepdims=True)
        o = acc_ref[0:tq, :] * inv_l[0:tq] - lam * (acc_ref[tq:, :] * inv_l[tq:])
        ms = jnp.mean(o * o, axis=-1, keepdims=True)
        o_ref[0, pl.ds(pl.multiple_of(qi * tq, tq), tq), :] = o * jax.lax.rsqrt(ms + LN_EPS) * out_gain
        return carry

    jax.lax.fori_loop(0, n_q, q_tile, 0)


def _attention(q, k, v, lq1, lk1, lq2, lk2, subln_g, lam_init, *, tq=256, tkv=512):
    b, s, d = q.shape
    tq = min(tq, s)
    tkv = min(tkv, s)
    head = pl.BlockSpec((1, s, V_DIM), lambda bi, h: (bi, 0, h))
    return pl.pallas_call(
        functools.partial(_attn_kernel, tq=tq, tkv=tkv, lam_init=lam_init),
        out_shape=jax.ShapeDtypeStruct((b, s, d), F32),
        grid=(b, N_HEADS),
        in_specs=[head, head, head] + [_const_spec(lq1.shape)] * 4 + [_const_spec(subln_g.shape)],
        out_specs=head,
        scratch_shapes=[pltpu.VMEM((s // tkv, 2 * tq, tkv), F32),
                        pltpu.VMEM((2 * tq, LANES), F32),
                        pltpu.VMEM((2 * tq, LANES), F32),
                        pltpu.VMEM((2 * tq, V_DIM), F32)],
        compiler_params=pltpu.CompilerParams(
            dimension_semantics=("arbitrary", "arbitrary"), vmem_limit_bytes=VMEM_LIMIT),
        name="diff_attn",
    )(q, k, v, lq1, lk1, lq2, lk2, subln_g)


def _post_kernel(c_ref, cp_ref, cn_ref, a_ref, g_ref, x_ref, dw_ref, dwb_ref, cg_ref, cb_ref,
                 pw_ref, wo_ref, g2_ref, b2_ref, o_ref, ext_ref):
    ts = c_ref.shape[1]
    si = pl.program_id(1)
    n_s = pl.num_programs(1)
    n_lt = D_MODEL // LANES
    prev_ok = (si > 0).astype(F32)
    next_ok = (si < n_s - 1).astype(F32)
    for j in range(n_lt):
        ls = slice(j * LANES, (j + 1) * LANES)
        ext_ref[j, 0:HALO, :] = cp_ref[0, :, ls] * prev_ok
        ext_ref[j, HALO:HALO + ts, :] = c_ref[0, :, ls]
        ext_ref[j, HALO + ts:, :] = cn_ref[0, :, ls] * next_ok
    pieces = []
    for j in range(n_lt):
        ls = slice(j * LANES, (j + 1) * LANES)
        acc = jnp.broadcast_to(dwb_ref[:, ls], (ts, LANES))
        for t in range(CONV_KERNEL):
            off = HALO - CONV_PAD + t
            acc = acc + dw_ref[t:t + 1, ls] * ext_ref[j, off:off + ts, :]
        pieces.append(acc)
    y = jnp.concatenate(pieces, axis=-1)
    y = _layer_norm(y, cg_ref[...], cb_ref[...])
    y = y * jax.nn.sigmoid(y)
    conv_out = _dot(y.astype(BF16), pw_ref[...])
    merged = g_ref[0, :, :D_MODEL] * a_ref[0] + g_ref[0, :, D_MODEL:] * conv_out
    mix = _dot(merged.astype(BF16), wo_ref[...])
    o_ref[0] = _layer_norm(ALPHA * x_ref[0] + mix, g2_ref[...], b2_ref[...])


def _post(c, attn, gates, x, dw, dwb, cg, cb, pw, wo, g2, b2, *, ts=256):
    b, s, d = c.shape
    ts = min(ts, s)
    hb = ts // HALO
    n_hb = s // HALO
    main = lambda w: pl.BlockSpec((1, ts, w), lambda bi, si: (bi, si, 0))
    prev = pl.BlockSpec((1, HALO, d), lambda bi, si: (bi, jnp.maximum(si * hb - 1, 0), 0))
    nxt = pl.BlockSpec((1, HALO, d), lambda bi, si: (bi, jnp.minimum((si + 1) * hb, n_hb - 1), 0))
    consts = [dw, dwb, cg, cb, pw, wo, g2, b2]
    return pl.pallas_call(
        _post_kernel,
        out_shape=jax.ShapeDtypeStruct((b, s, d), F32),
        grid=(b, s // ts),
        in_specs=[main(d), prev, nxt, main(d), main(2 * d), main(d)] + [_const_spec(w.shape) for w in consts],
        out_specs=main(d),
        scratch_shapes=[pltpu.VMEM((d // LANES, ts + 2 * HALO, LANES), F32)],
        compiler_params=pltpu.CompilerParams(
            dimension_semantics=("arbitrary", "arbitrary"), vmem_limit_bytes=VMEM_LIMIT),
        name="conv_merge_out",
    )(c, c, c, attn, gates, x, *consts)


def _trunk(x, p, w):
    b, s, d = x.shape
    t = b * s
    cos, sin = _rope_tables(s)
    x = x.reshape(t, d)
    row = lambda a: a.reshape(1, -1)
    for i in range(DEPTH):
        lam_init = 0.8 - 0.6 * math.exp(-0.3 * i)
        x = _ffn(x, w["ffn1_w_gate"][i], w["ffn1_w_up"][i], w["ffn1_w_down"][i],
                 row(w["ln1_g"][i]), row(w["ln1_b"][i]))
        q, k, v, c, g = _inproj(x, w["w_in"][i], cos, sin, s)
        shp = lambda a: a.reshape(b, s, -1)
        attn = _attention(shp(q), shp(k), shp(v), row(w["lam_q1"][i]), row(w["lam_k1"][i]),
                          row(w["lam_q2"][i]), row(w["lam_k2"][i]), row(w["subln_g"][i]), lam_init)
        x = _post(shp(c), attn, shp(g), shp(x), w["conv_dw"][i], row(w["conv_dw_b"][i]),
                  row(w["conv_ln_g"][i]), row(w["conv_ln_b"][i]), w["conv_pw2"][i], w["w_o"][i],
                  row(w["ln2_g"][i]), row(w["ln2_b"][i])).reshape(t, d)
        x = _ffn(x, w["ffn2_w_gate"][i], w["ffn2_w_up"][i], w["ffn2_w_down"][i],
                 row(w["ln3_g"][i]), row(w["ln3_b"][i]),
                 ple=(p[i].reshape(t, PLE_DIM), w["ple_w_gate"][i], w["ple_w_proj"][i]))
    return x.reshape(b, s, d)


_MATMUL_WEIGHTS = ("ffn1_w_gate", "ffn1_w_up", "ffn1_w_down", "w_in", "conv_pw2", "w_o",
                   "ffn2_w_gate", "ffn2_w_up", "ffn2_w_down", "ple_w_gate", "ple_w_proj")


def kernel(x_prompt, x_sample, p_prompt, p_sample, ffn1_w_gate, ffn1_w_up, ffn1_w_down, ln1_g, ln1_b, w_in, lam_q1, lam_k1, lam_q2, lam_k2, subln_g, conv_dw, conv_dw_b, conv_ln_g, conv_ln_b, conv_pw2, w_o, ln2_g, ln2_b, ffn2_w_gate, ffn2_w_up, ffn2_w_down, ple_w_gate, ple_w_proj, ln3_g, ln3_b):
    w = dict(ffn1_w_gate=ffn1_w_gate, ffn1_w_up=ffn1_w_up, ffn1_w_down=ffn1_w_down, ln1_g=ln1_g,
             ln1_b=ln1_b, w_in=w_in, lam_q1=lam_q1, lam_k1=lam_k1, lam_q2=lam_q2, lam_k2=lam_k2,
             subln_g=subln_g, conv_dw=conv_dw, conv_dw_b=conv_dw_b, conv_ln_g=conv_ln_g,
             conv_ln_b=conv_ln_b, conv_pw2=conv_pw2, w_o=w_o, ln2_g=ln2_g, ln2_b=ln2_b,
             ffn2_w_gate=ffn2_w_gate, ffn2_w_up=ffn2_w_up, ffn2_w_down=ffn2_w_down,
             ple_w_gate=ple_w_gate, ple_w_proj=ple_w_proj, ln3_g=ln3_g, ln3_b=ln3_b)
    for name in _MATMUL_WEIGHTS:
        w[name] = w[name].astype(BF16)
    return _trunk(x_prompt, p_prompt, w), _trunk(x_sample, p_sample, w)
```

```python
import functools
import math

import jax
import jax.numpy as jnp
from jax.experimental import pallas as pl
from jax.experimental.pallas import tpu as pltpu

D_MODEL = 1024
DEPTH = 4
PLE_DIM = 256
N_HEADS = 8
HEAD_DIM = 64
V_DIM = 2 * HEAD_DIM
CONV_KERNEL = 31
CONV_PAD = CONV_KERNEL // 2
D_FF = 2816
ROPE_THETA = 10000.0
LN_EPS = 1e-5
ALPHA = (2 * DEPTH) ** 0.25

LANES = 128
HALO = 16
VMEM_LIMIT = 56 * 1024 * 1024

BF16 = jnp.bfloat16
F32 = jnp.float32


def _dot(a, b):
    return jnp.dot(a, b, preferred_element_type=F32)


def _dot_nt(a, b):
    return jax.lax.dot_general(a, b, (((1,), (1,)), ((), ())), preferred_element_type=F32)


def _layer_norm(y, g, b):
    mu = jnp.mean(y, axis=-1, keepdims=True)
    d = y - mu
    var = jnp.mean(d * d, axis=-1, keepdims=True)
    return d * jax.lax.rsqrt(var + LN_EPS) * g + b


def _const_spec(shape):
    return pl.BlockSpec(shape, lambda *_: (0,) * len(shape), pipeline_mode=pl.Buffered(1))


FF_CHUNK = 256


def _ffn_kernel(*refs, with_ple):
    if with_ple:
        (x_ref, p_ref, wg_ref, wu_ref, wd_ref, pg_ref, pp_ref, g_ref, b_ref, o_ref) = refs
    else:
        (x_ref, wg_ref, wu_ref, wd_ref, g_ref, b_ref, o_ref) = refs
    x = x_ref[...]
    xb = x.astype(BF16)
    acc = jnp.zeros(x.shape, F32)
    for f in range(0, D_FF, FF_CHUNK):
        hg = _dot(xb, wg_ref[:, f:f + FF_CHUNK])
        hu = _dot(xb, wu_ref[:, f:f + FF_CHUNK])
        h = (hg * jax.nn.sigmoid(hg)) * hu
        acc = acc + _dot(h.astype(BF16), wd_ref[f:f + FF_CHUNK, :])
    y = ALPHA * x + 0.5 * acc
    if with_ple:
        gate = jax.nn.sigmoid(_dot(xb, pg_ref[...]))
        y = y + gate * _dot(p_ref[...].astype(BF16), pp_ref[...])
    o_ref[...] = _layer_norm(y, g_ref[...], b_ref[...])


def _ffn(x, wg, wu, wd, g, b, ple=None, *, tm=512):
    t, d = x.shape
    row = lambda w: pl.BlockSpec((tm, w), lambda i: (i, 0))
    ins = [x]
    specs = [row(d)]
    if ple is not None:
        p, pg, pp = ple
        ins.append(p)
        specs.append(row(PLE_DIM))
    ins += [wg, wu, wd]
    specs += [_const_spec(wg.shape), _const_spec(wu.shape), _const_spec(wd.shape)]
    if ple is not None:
        ins += [pg, pp]
        specs += [_const_spec(pg.shape), _const_spec(pp.shape)]
    ins += [g, b]
    specs += [_const_spec(g.shape), _const_spec(b.shape)]
    return pl.pallas_call(
        functools.partial(_ffn_kernel, with_ple=ple is not None),
        out_shape=jax.ShapeDtypeStruct((t, d), F32),
        grid=(t // tm,),
        in_specs=specs,
        out_specs=row(d),
        compiler_params=pltpu.CompilerParams(
            dimension_semantics=("arbitrary",), vmem_limit_bytes=VMEM_LIMIT),
        name="ffn_ple" if ple is not None else "ffn",
    )(*ins)


QK_SCALE = HEAD_DIM ** -0.5 * math.log2(math.e)


def _rope_tables(seq_len):
    half = HEAD_DIM // 2
    inv_freq = ROPE_THETA ** (-jnp.arange(half, dtype=F32) / half)
    ang = jnp.arange(seq_len, dtype=F32)[:, None] * inv_freq[None, :]
    cos, sin = jnp.cos(ang), jnp.sin(ang)
    cos = jnp.concatenate([cos, cos], axis=-1)
    sin = jnp.concatenate([-sin, sin], axis=-1)
    reps = LANES // HEAD_DIM
    return jnp.tile(cos, (1, reps)), jnp.tile(sin, (1, reps))


def _rope(z, cos, sin, first_half):
    half = HEAD_DIM // 2
    fwd = pltpu.roll(z, LANES - half, axis=1)
    bwd = pltpu.roll(z, half, axis=1)
    return z * cos + jnp.where(first_half, fwd, bwd) * sin


def _inproj_kernel(x_ref, w_ref, cos_ref, sin_ref, q_ref, k_ref, v_ref, c_ref, g_ref):
    xb = x_ref[...].astype(BF16)
    cos = cos_ref[...]
    sin = sin_ref[...]
    lane = jax.lax.broadcasted_iota(jnp.int32, cos.shape, 1)
    first_half = (lane % HEAD_DIM) < (HEAD_DIM // 2)
    d = D_MODEL
    for j in range(d // LANES):
        zq = _dot(xb, w_ref[:, j * LANES:(j + 1) * LANES])
        q_ref[:, j * LANES:(j + 1) * LANES] = (_rope(zq, cos, sin, first_half) * QK_SCALE).astype(BF16)
        zk = _dot(xb, w_ref[:, d + j * LANES:d + (j + 1) * LANES])
        k_ref[:, j * LANES:(j + 1) * LANES] = _rope(zk, cos, sin, first_half).astype(BF16)
    v_ref[...] = _dot(xb, w_ref[:, 2 * d:3 * d]).astype(BF16)
    ua = _dot(xb, w_ref[:, 3 * d:4 * d])
    ub = _dot(xb, w_ref[:, 4 * d:5 * d])
    c_ref[...] = ua * jax.nn.sigmoid(ub)
    g_ref[...] = jax.nn.sigmoid(_dot(xb, w_ref[:, 5 * d:7 * d]))


def _inproj(x, w_in, cos, sin, seq_len, *, tm=512):
    t, d = x.shape
    per_seq = seq_len // tm
    row = lambda w: pl.BlockSpec((tm, w), lambda i: (i, 0))
    pos = pl.BlockSpec((tm, LANES), lambda i: (i % per_seq, 0))
    return pl.pallas_call(
        _inproj_kernel,
        out_shape=(jax.ShapeDtypeStruct((t, d), BF16), jax.ShapeDtypeStruct((t, d), BF16),
                   jax.ShapeDtypeStruct((t, d), BF16), jax.ShapeDtypeStruct((t, d), F32),
                   jax.ShapeDtypeStruct((t, 2 * d), F32)),
        grid=(t // tm,),
        in_specs=[row(d), _const_spec(w_in.shape), pos, pos],
        out_specs=(row(d), row(d), row(d), row(d), row(2 * d)),
        compiler_params=pltpu.CompilerParams(
            dimension_semantics=("arbitrary",), vmem_limit_bytes=VMEM_LIMIT),
        name="inproj",
    )(x, w_in, cos, sin)


CHUNK_UNROLL = 8


def _attn_kernel(q_ref, k_ref, v_ref, lq1_ref, lk1_ref, lq2_ref, lk2_ref, sg_ref, o_ref,
                 s_ref, m_ref, mn_ref, l_ref, acc_ref, *, tq, tkv, lam_init):
    seq = k_ref.shape[1]
    n_q = seq // tq
    n_c = seq // tkv
    n_lt = tkv // LANES
    unroll = min(CHUNK_UNROLL, n_c)
    map0 = jax.lax.broadcasted_iota(jnp.int32, (tq, LANES), 1) < HEAD_DIM
    lam = (jnp.exp(jnp.sum(lq1_ref[...] * lk1_ref[...], axis=-1, keepdims=True))
           - jnp.exp(jnp.sum(lq2_ref[...] * lk2_ref[...], axis=-1, keepdims=True)) + lam_init)
    out_gain = sg_ref[...] * (1.0 - lam_init)

    def stacked_q(qi):
        q = q_ref[0, pl.ds(pl.multiple_of(qi * tq, tq), tq), :]
        zero = jnp.zeros_like(q)
        return jnp.concatenate([jnp.where(map0, q, zero), jnp.where(map0, zero, q)], axis=0)

    def scores(qs, c):
        kc = k_ref[0, pl.ds(pl.multiple_of(c * tkv, tkv), tkv), :]
        s = _dot_nt(qs, kc)
        s_ref[c] = s
        m = mn_ref[...]
        for j in range(n_lt):
            m = jnp.maximum(m, s[:, j * LANES:(j + 1) * LANES])
        mn_ref[...] = m

    def weighted(c):
        vc = v_ref[0, pl.ds(pl.multiple_of(c * tkv, tkv), tkv), :]
        m = m_ref[...]
        l = l_ref[...]
        ps = []
        for j in range(n_lt):
            p = jnp.exp2(s_ref[c, :, j * LANES:(j + 1) * LANES] - m)
            l = l + p
            ps.append(p.astype(BF16))
        l_ref[...] = l
        acc_ref[...] += _dot(jnp.concatenate(ps, axis=-1), vc)

    def start_tile():
        m_ref[...] = jnp.broadcast_to(jnp.max(mn_ref[...], axis=-1, keepdims=True), m_ref.shape)
        mn_ref[...] = jnp.full(mn_ref.shape, -jnp.inf, F32)
        l_ref[...] = jnp.zeros(l_ref.shape, F32)
        acc_ref[...] = jnp.zeros(acc_ref.shape, F32)

    def finish_tile(qi):
        inv_l = 1.0 / jnp.sum(l_ref[...], axis=-1, keepdims=True)
        o = acc_ref[0:tq, :] * inv_l[0:tq] - lam * (acc_ref[tq:, :] * inv_l[tq:])
        ms = jnp.mean(o * o, axis=-1, keepdims=True)
        o_ref[0, pl.ds(pl.multiple_of(qi * tq, tq), tq), :] = o * jax.lax.rsqrt(ms + LN_EPS) * out_gain

    def chunk_loop(body):
        def step(c, carry):
            body(c)
            return carry
        jax.lax.fori_loop(0, n_c, step, 0, unroll=unroll)

    mn_ref[...] = jnp.full(mn_ref.shape, -jnp.inf, F32)
    qs0 = stacked_q(0)
    chunk_loop(lambda c: scores(qs0, c))
    start_tile()

    def q_tile(qi, carry):
        qs = stacked_q(qi + 1)

        def both(c):
            weighted(c)
            scores(qs, c)

        chunk_loop(both)
        finish_tile(qi)
        start_tile()
        return carry

    jax.lax.fori_loop(0, n_q - 1, q_tile, 0)
    chunk_loop(weighted)
    finish_tile(n_q - 1)


def _attention(q, k, v, lq1, lk1, lq2, lk2, subln_g, lam_init, *, tq=256, tkv=512):
    b, s, d = q.shape
    tq = min(tq, s)
    tkv = min(tkv, s)
    head = pl.BlockSpec((1, s, V_DIM), lambda bi, h: (bi, 0, h))
    return pl.pallas_call(
        functools.partial(_attn_kernel, tq=tq, tkv=tkv, lam_init=lam_init),
        out_shape=jax.ShapeDtypeStruct((b, s, d), F32),
        grid=(b, N_HEADS),
        in_specs=[head, head, head] + [_const_spec(lq1.shape)] * 4 + [_const_spec(subln_g.shape)],
        out_specs=head,
        scratch_shapes=[pltpu.VMEM((s // tkv, 2 * tq, tkv), F32),
                        pltpu.VMEM((2 * tq, LANES), F32),
                        pltpu.VMEM((2 * tq, LANES), F32),
                        pltpu.VMEM((2 * tq, LANES), F32),
                        pltpu.VMEM((2 * tq, V_DIM), F32)],
        compiler_params=pltpu.CompilerParams(
            dimension_semantics=("arbitrary", "arbitrary"), vmem_limit_bytes=VMEM_LIMIT),
        name="diff_attn",
    )(q, k, v, lq1, lk1, lq2, lk2, subln_g)


def _post_kernel(c_ref, cp_ref, cn_ref, a_ref, g_ref, x_ref, dw_ref, dwb_ref, cg_ref, cb_ref,
                 pw_ref, wo_ref, g2_ref, b2_ref, o_ref, ext_ref):
    ts = c_ref.shape[1]
    si = pl.program_id(1)
    n_s = pl.num_programs(1)
    n_lt = D_MODEL // LANES
    prev_ok = (si > 0).astype(F32)
    next_ok = (si < n_s - 1).astype(F32)
    for j in range(n_lt):
        ls = slice(j * LANES, (j + 1) * LANES)
        ext_ref[j, 0:HALO, :] = cp_ref[0, :, ls] * prev_ok
        ext_ref[j, HALO:HALO + ts, :] = c_ref[0, :, ls]
        ext_ref[j, HALO + ts:, :] = cn_ref[0, :, ls] * next_ok
    pieces = []
    for j in range(n_lt):
        ls = slice(j * LANES, (j + 1) * LANES)
        acc = jnp.broadcast_to(dwb_ref[:, ls], (ts, LANES))
        for t in range(CONV_KERNEL):
            off = HALO - CONV_PAD + t
            acc = acc + dw_ref[t:t + 1, ls] * ext_ref[j, off:off + ts, :]
        pieces.append(acc)
    y = jnp.concatenate(pieces, axis=-1)
    y = _layer_norm(y, cg_ref[...], cb_ref[...])
    y = y * jax.nn.sigmoid(y)
    conv_out = _dot(y.astype(BF16), pw_ref[...])
    merged = g_ref[0, :, :D_MODEL] * a_ref[0] + g_ref[0, :, D_MODEL:] * conv_out
    mix = _dot(merged.astype(BF16), wo_ref[...])
    o_ref[0] = _layer_norm(ALPHA * x_ref[0] + mix, g2_ref[...], b2_ref[...])


def _post(c, attn, gates, x, dw, dwb, cg, cb, pw, wo, g2, b2, *, ts=256):
    b, s, d = c.shape
    ts = min(ts, s)
    hb = ts // HALO
    n_hb = s // HALO
    main = lambda w: pl.BlockSpec((1, ts, w), lambda bi, si: (bi, si, 0))
    prev = pl.BlockSpec((1, HALO, d), lambda bi, si: (bi, jnp.maximum(si * hb - 1, 0), 0))
    nxt = pl.BlockSpec((1, HALO, d), lambda bi, si: (bi, jnp.minimum((si + 1) * hb, n_hb - 1), 0))
    consts = [dw, dwb, cg, cb, pw, wo, g2, b2]
    return pl.pallas_call(
        _post_kernel,
        out_shape=jax.ShapeDtypeStruct((b, s, d), F32),
        grid=(b, s // ts),
        in_specs=[main(d), prev, nxt, main(d), main(2 * d), main(d)] + [_const_spec(w.shape) for w in consts],
        out_specs=main(d),
        scratch_shapes=[pltpu.VMEM((d // LANES, ts + 2 * HALO, LANES), F32)],
        compiler_params=pltpu.CompilerParams(
            dimension_semantics=("arbitrary", "arbitrary"), vmem_limit_bytes=VMEM_LIMIT),
        name="conv_merge_out",
    )(c, c, c, attn, gates, x, *consts)


def _trunk(x, p, w):
    b, s, d = x.shape
    t = b * s
    cos, sin = _rope_tables(s)
    x = x.reshape(t, d)
    row = lambda a: a.reshape(1, -1)
    for i in range(DEPTH):
        lam_init = 0.8 - 0.6 * math.exp(-0.3 * i)
        x = _ffn(x, w["ffn1_w_gate"][i], w["ffn1_w_up"][i], w["ffn1_w_down"][i],
                 row(w["ln1_g"][i]), row(w["ln1_b"][i]))
        q, k, v, c, g = _inproj(x, w["w_in"][i], cos, sin, s)
        shp = lambda a: a.reshape(b, s, -1)
        attn = _attention(shp(q), shp(k), shp(v), row(w["lam_q1"][i]), row(w["lam_k1"][i]),
                          row(w["lam_q2"][i]), row(w["lam_k2"][i]), row(w["subln_g"][i]), lam_init)
        x = _post(shp(c), attn, shp(g), shp(x), w["conv_dw"][i], row(w["conv_dw_b"][i]),
                  row(w["conv_ln_g"][i]), row(w["conv_ln_b"][i]), w["conv_pw2"][i], w["w_o"][i],
                  row(w["ln2_g"][i]), row(w["ln2_b"][i])).reshape(t, d)
        x = _ffn(x, w["ffn2_w_gate"][i], w["ffn2_w_up"][i], w["ffn2_w_down"][i],
                 row(w["ln3_g"][i]), row(w["ln3_b"][i]),
                 ple=(p[i].reshape(t, PLE_DIM), w["ple_w_gate"][i], w["ple_w_proj"][i]))
    return x.reshape(b, s, d)


_MATMUL_WEIGHTS = ("ffn1_w_gate", "ffn1_w_up", "ffn1_w_down", "w_in", "conv_pw2", "w_o",
                   "ffn2_w_gate", "ffn2_w_up", "ffn2_w_down", "ple_w_gate", "ple_w_proj")


def kernel(x_prompt, x_sample, p_prompt, p_sample, ffn1_w_gate, ffn1_w_up, ffn1_w_down, ln1_g, ln1_b, w_in, lam_q1, lam_k1, lam_q2, lam_k2, subln_g, conv_dw, conv_dw_b, conv_ln_g, conv_ln_b, conv_pw2, w_o, ln2_g, ln2_b, ffn2_w_gate, ffn2_w_up, ffn2_w_down, ple_w_gate, ple_w_proj, ln3_g, ln3_b):
    w = dict(ffn1_w_gate=ffn1_w_gate, ffn1_w_up=ffn1_w_up, ffn1_w_down=ffn1_w_down, ln1_g=ln1_g,
             ln1_b=ln1_b, w_in=w_in, lam_q1=lam_q1, lam_k1=lam_k1, lam_q2=lam_q2, lam_k2=lam_k2,
             subln_g=subln_g, conv_dw=conv_dw, conv_dw_b=conv_dw_b, conv_ln_g=conv_ln_g,
             conv_ln_b=conv_ln_b, conv_pw2=conv_pw2, w_o=w_o, ln2_g=ln2_g, ln2_b=ln2_b,
             ffn2_w_gate=ffn2_w_gate, ffn2_w_up=ffn2_w_up, ffn2_w_down=ffn2_w_down,
             ple_w_gate=ple_w_gate, ple_w_proj=ple_w_proj, ln3_g=ln3_g, ln3_b=ln3_b)
    for name in _MATMUL_WEIGHTS:
        w[name] = w[name].astype(BF16)
    return _trunk(x_prompt, p_prompt, w), _trunk(x_sample, p_sample, w)
```

```python
import functools
import math

import jax
import jax.numpy as jnp
from jax.experimental import pallas as pl
from jax.experimental.pallas import tpu as pltpu

D_MODEL = 1024
DEPTH = 4
PLE_DIM = 256
N_HEADS = 8
HEAD_DIM = 64
V_DIM = 2 * HEAD_DIM
CONV_KERNEL = 31
CONV_PAD = CONV_KERNEL // 2
D_FF = 2816
ROPE_THETA = 10000.0
LN_EPS = 1e-5
ALPHA = (2 * DEPTH) ** 0.25

LANES = 128
MXU_COLS = 256
HALO = 16
VMEM_LIMIT = 56 * 1024 * 1024

BF16 = jnp.bfloat16
F32 = jnp.float32


def _dot(a, b):
    return jnp.dot(a, b, preferred_element_type=F32)


def _dot_nt(a, b):
    return jax.lax.dot_general(a, b, (((1,), (1,)), ((), ())), preferred_element_type=F32)


def _layer_norm(y, g, b):
    mu = jnp.mean(y, axis=-1, keepdims=True)
    d = y - mu
    var = jnp.mean(d * d, axis=-1, keepdims=True)
    return d * jax.lax.rsqrt(var + LN_EPS) * g + b


def _const_spec(shape):
    return pl.BlockSpec(shape, lambda *_: (0,) * len(shape), pipeline_mode=pl.Buffered(1))


FF_CHUNK = 256


def _ffn_kernel(*refs, with_ple):
    if with_ple:
        (x_ref, p_ref, wg_ref, wu_ref, wd_ref, pg_ref, pp_ref, g_ref, b_ref, o_ref) = refs
    else:
        (x_ref, wg_ref, wu_ref, wd_ref, g_ref, b_ref, o_ref) = refs
    x = x_ref[...]
    xb = x.astype(BF16)
    acc = jnp.zeros(x.shape, F32)
    for f in range(0, D_FF, FF_CHUNK):
        hg = _dot(xb, wg_ref[:, f:f + FF_CHUNK])
        hu = _dot(xb, wu_ref[:, f:f + FF_CHUNK])
        h = (hg * jax.nn.sigmoid(hg)) * hu
        acc = acc + _dot(h.astype(BF16), wd_ref[f:f + FF_CHUNK, :])
    y = ALPHA * x + 0.5 * acc
    if with_ple:
        gate = jax.nn.sigmoid(_dot(xb, pg_ref[...]))
        y = y + gate * _dot(p_ref[...].astype(BF16), pp_ref[...])
    o_ref[...] = _layer_norm(y, g_ref[...], b_ref[...])


def _ffn(x, wg, wu, wd, g, b, ple=None, *, tm=512):
    t, d = x.shape
    row = lambda w: pl.BlockSpec((tm, w), lambda i: (i, 0))
    ins = [x]
    specs = [row(d)]
    if ple is not None:
        p, pg, pp = ple
        ins.append(p)
        specs.append(row(PLE_DIM))
    ins += [wg, wu, wd]
    specs += [_const_spec(wg.shape), _const_spec(wu.shape), _const_spec(wd.shape)]
    if ple is not None:
        ins += [pg, pp]
        specs += [_const_spec(pg.shape), _const_spec(pp.shape)]
    ins += [g, b]
    specs += [_const_spec(g.shape), _const_spec(b.shape)]
    return pl.pallas_call(
        functools.partial(_ffn_kernel, with_ple=ple is not None),
        out_shape=jax.ShapeDtypeStruct((t, d), F32),
        grid=(t // tm,),
        in_specs=specs,
        out_specs=row(d),
        compiler_params=pltpu.CompilerParams(
            dimension_semantics=("arbitrary",), vmem_limit_bytes=VMEM_LIMIT),
        name="ffn_ple" if ple is not None else "ffn",
    )(*ins)


QK_SCALE = HEAD_DIM ** -0.5 * math.log2(math.e)


def _rope_tables(seq_len):
    half = HEAD_DIM // 2
    inv_freq = ROPE_THETA ** (-jnp.arange(half, dtype=F32) / half)
    ang = jnp.arange(seq_len, dtype=F32)[:, None] * inv_freq[None, :]
    cos, sin = jnp.cos(ang), jnp.sin(ang)
    cos = jnp.concatenate([cos, cos], axis=-1)
    sin = jnp.concatenate([-sin, sin], axis=-1)
    reps = LANES // HEAD_DIM
    return jnp.tile(cos, (1, reps)), jnp.tile(sin, (1, reps))


def _rope(z, cos, sin, first_half):
    half = HEAD_DIM // 2
    fwd = pltpu.roll(z, LANES - half, axis=1)
    bwd = pltpu.roll(z, half, axis=1)
    return z * cos + jnp.where(first_half, fwd, bwd) * sin


def _inproj_kernel(x_ref, w_ref, cos_ref, sin_ref, q_ref, k_ref, v_ref, c_ref, g_ref):
    xb = x_ref[...].astype(BF16)
    cos = cos_ref[...]
    sin = sin_ref[...]
    lane = jax.lax.broadcasted_iota(jnp.int32, cos.shape, 1)
    first_half = (lane % HEAD_DIM) < (HEAD_DIM // 2)
    d = D_MODEL
    for j in range(0, d, MXU_COLS):
        zq = _dot(xb, w_ref[:, j:j + MXU_COLS])
        zk = _dot(xb, w_ref[:, d + j:d + j + MXU_COLS])
        for t in range(0, MXU_COLS, LANES):
            q_ref[:, j + t:j + t + LANES] = (
                _rope(zq[:, t:t + LANES], cos, sin, first_half) * QK_SCALE).astype(BF16)
            k_ref[:, j + t:j + t + LANES] = _rope(zk[:, t:t + LANES], cos, sin, first_half).astype(BF16)
    v_ref[...] = _dot(xb, w_ref[:, 2 * d:3 * d]).astype(BF16)
    ua = _dot(xb, w_ref[:, 3 * d:4 * d])
    ub = _dot(xb, w_ref[:, 4 * d:5 * d])
    c_ref[...] = ua * jax.nn.sigmoid(ub)
    g_ref[...] = jax.nn.sigmoid(_dot(xb, w_ref[:, 5 * d:7 * d]))


def _inproj(x, w_in, cos, sin, seq_len, *, tm=512):
    t, d = x.shape
    per_seq = seq_len // tm
    row = lambda w: pl.BlockSpec((tm, w), lambda i: (i, 0))
    pos = pl.BlockSpec((tm, LANES), lambda i: (i % per_seq, 0))
    return pl.pallas_call(
        _inproj_kernel,
        out_shape=(jax.ShapeDtypeStruct((t, d), BF16), jax.ShapeDtypeStruct((t, d), BF16),
                   jax.ShapeDtypeStruct((t, d), BF16), jax.ShapeDtypeStruct((t, d), F32),
                   jax.ShapeDtypeStruct((t, 2 * d), F32)),
        grid=(t // tm,),
        in_specs=[row(d), _const_spec(w_in.shape), pos, pos],
        out_specs=(row(d), row(d), row(d), row(d), row(2 * d)),
        compiler_params=pltpu.CompilerParams(
            dimension_semantics=("arbitrary",), vmem_limit_bytes=VMEM_LIMIT),
        name="inproj",
    )(x, w_in, cos, sin)


CHUNK_UNROLL = 8


def _attn_kernel(q_ref, k_ref, v_ref, lq1_ref, lk1_ref, lq2_ref, lk2_ref, sg_ref, o_ref,
                 s_ref, m_ref, mn_ref, l_ref, acc_ref, *, tq, tkv, lam_init):
    seq = k_ref.shape[1]
    n_q = seq // tq
    n_c = seq // tkv
    n_lt = tkv // LANES
    unroll = min(CHUNK_UNROLL, n_c)
    map0 = jax.lax.broadcasted_iota(jnp.int32, (tq, LANES), 1) < HEAD_DIM
    lam = (jnp.exp(jnp.sum(lq1_ref[...] * lk1_ref[...], axis=-1, keepdims=True))
           - jnp.exp(jnp.sum(lq2_ref[...] * lk2_ref[...], axis=-1, keepdims=True)) + lam_init)
    out_gain = sg_ref[...] * (1.0 - lam_init)

    def stacked_q(qi):
        q = q_ref[0, pl.ds(pl.multiple_of(qi * tq, tq), tq), :]
        zero = jnp.zeros_like(q)
        return jnp.concatenate([jnp.where(map0, q, zero), jnp.where(map0, zero, q)], axis=0)

    def scores(qs, c):
        kc = k_ref[0, pl.ds(pl.multiple_of(c * tkv, tkv), tkv), :]
        s = _dot_nt(qs, kc)
        s_ref[c] = s
        m = mn_ref[...]
        for j in range(n_lt):
            m = jnp.maximum(m, s[:, j * LANES:(j + 1) * LANES])
        mn_ref[...] = m

    def weighted(c):
        vc = v_ref[0, pl.ds(pl.multiple_of(c * tkv, tkv), tkv), :]
        m = m_ref[...]
        l = l_ref[...]
        ps = []
        for j in range(n_lt):
            p = jnp.exp2(s_ref[c, :, j * LANES:(j + 1) * LANES] - m)
            l = l + p
            ps.append(p.astype(BF16))
        l_ref[...] = l
        acc_ref[...] += _dot(jnp.concatenate(ps, axis=-1), vc)

    def start_tile():
        m_ref[...] = jnp.broadcast_to(jnp.max(mn_ref[...], axis=-1, keepdims=True), m_ref.shape)
        mn_ref[...] = jnp.full(mn_ref.shape, -jnp.inf, F32)
        l_ref[...] = jnp.zeros(l_ref.shape, F32)
        acc_ref[...] = jnp.zeros(acc_ref.shape, F32)

    def finish_tile(qi):
        inv_l = 1.0 / jnp.sum(l_ref[...], axis=-1, keepdims=True)
        o = acc_ref[0:tq, :] * inv_l[0:tq] - lam * (acc_ref[tq:, :] * inv_l[tq:])
        ms = jnp.mean(o * o, axis=-1, keepdims=True)
        o_ref[0, pl.ds(pl.multiple_of(qi * tq, tq), tq), :] = o * jax.lax.rsqrt(ms + LN_EPS) * out_gain

    def chunk_loop(body):
        def step(c, carry):
            body(c)
            return carry
        jax.lax.fori_loop(0, n_c, step, 0, unroll=unroll)

    mn_ref[...] = jnp.full(mn_ref.shape, -jnp.inf, F32)
    qs0 = stacked_q(0)
    chunk_loop(lambda c: scores(qs0, c))
    start_tile()

    def q_tile(qi, carry):
        qs = stacked_q(qi + 1)

        def both(c):
            weighted(c)
            scores(qs, c)

        chunk_loop(both)
        finish_tile(qi)
        start_tile()
        return carry

    jax.lax.fori_loop(0, n_q - 1, q_tile, 0)
    chunk_loop(weighted)
    finish_tile(n_q - 1)


def _attention(q, k, v, lq1, lk1, lq2, lk2, subln_g, lam_init, *, tq=256, tkv=512):
    b, s, d = q.shape
    tq = min(tq, s)
    tkv = min(tkv, s)
    head = pl.BlockSpec((1, s, V_DIM), lambda bi, h: (bi, 0, h))
    return pl.pallas_call(
        functools.partial(_attn_kernel, tq=tq, tkv=tkv, lam_init=lam_init),
        out_shape=jax.ShapeDtypeStruct((b, s, d), F32),
        grid=(b, N_HEADS),
        in_specs=[head, head, head] + [_const_spec(lq1.shape)] * 4 + [_const_spec(subln_g.shape)],
        out_specs=head,
        scratch_shapes=[pltpu.VMEM((s // tkv, 2 * tq, tkv), F32),
                        pltpu.VMEM((2 * tq, LANES), F32),
                        pltpu.VMEM((2 * tq, LANES), F32),
                        pltpu.VMEM((2 * tq, LANES), F32),
                        pltpu.VMEM((2 * tq, V_DIM), F32)],
        compiler_params=pltpu.CompilerParams(
            dimension_semantics=("arbitrary", "arbitrary"), vmem_limit_bytes=VMEM_LIMIT),
        name="diff_attn",
    )(q, k, v, lq1, lk1, lq2, lk2, subln_g)


def _post_kernel(c_ref, cp_ref, cn_ref, a_ref, g_ref, x_ref, dw_ref, dwb_ref, cg_ref, cb_ref,
                 pw_ref, wo_ref, g2_ref, b2_ref, o_ref, ext_ref):
    ts = c_ref.shape[1]
    si = pl.program_id(1)
    n_s = pl.num_programs(1)
    n_lt = D_MODEL // LANES
    prev_ok = (si > 0).astype(F32)
    next_ok = (si < n_s - 1).astype(F32)
    for j in range(n_lt):
        ls = slice(j * LANES, (j + 1) * LANES)
        ext_ref[j, 0:HALO, :] = cp_ref[0, :, ls] * prev_ok
        ext_ref[j, HALO:HALO + ts, :] = c_ref[0, :, ls]
        ext_ref[j, HALO + ts:, :] = cn_ref[0, :, ls] * next_ok
    pieces = []
    for j in range(n_lt):
        ls = slice(j * LANES, (j + 1) * LANES)
        acc = jnp.broadcast_to(dwb_ref[:, ls], (ts, LANES))
        for t in range(CONV_KERNEL):
            off = HALO - CONV_PAD + t
            acc = acc + dw_ref[t:t + 1, ls] * ext_ref[j, off:off + ts, :]
        pieces.append(acc)
    y = jnp.concatenate(pieces, axis=-1)
    y = _layer_norm(y, cg_ref[...], cb_ref[...])
    y = y * jax.nn.sigmoid(y)
    conv_out = _dot(y.astype(BF16), pw_ref[...])
    merged = g_ref[0, :, :D_MODEL] * a_ref[0] + g_ref[0, :, D_MODEL:] * conv_out
    mix = _dot(merged.astype(BF16), wo_ref[...])
    o_ref[0] = _layer_norm(ALPHA * x_ref[0] + mix, g2_ref[...], b2_ref[...])


def _post(c, attn, gates, x, dw, dwb, cg, cb, pw, wo, g2, b2, *, ts=256):
    b, s, d = c.shape
    ts = min(ts, s)
    hb = ts // HALO
    n_hb = s // HALO
    main = lambda w: pl.BlockSpec((1, ts, w), lambda bi, si: (bi, si, 0))
    prev = pl.BlockSpec((1, HALO, d), lambda bi, si: (bi, jnp.maximum(si * hb - 1, 0), 0))
    nxt = pl.BlockSpec((1, HALO, d), lambda bi, si: (bi, jnp.minimum((si + 1) * hb, n_hb - 1), 0))
    consts = [dw, dwb, cg, cb, pw, wo, g2, b2]
    return pl.pallas_call(
        _post_kernel,
        out_shape=jax.ShapeDtypeStruct((b, s, d), F32),
        grid=(b, s // ts),
        in_specs=[main(d), prev, nxt, main(d), main(2 * d), main(d)] + [_const_spec(w.shape) for w in consts],
        out_specs=main(d),
        scratch_shapes=[pltpu.VMEM((d // LANES, ts + 2 * HALO, LANES), F32)],
        compiler_params=pltpu.CompilerParams(
            dimension_semantics=("arbitrary", "arbitrary"), vmem_limit_bytes=VMEM_LIMIT),
        name="conv_merge_out",
    )(c, c, c, attn, gates, x, *consts)


def _trunk(x, p, w):
    b, s, d = x.shape
    t = b * s
    cos, sin = _rope_tables(s)
    x = x.reshape(t, d)
    row = lambda a: a.reshape(1, -1)
    for i in range(DEPTH):
        lam_init = 0.8 - 0.6 * math.exp(-0.3 * i)
        x = _ffn(x, w["ffn1_w_gate"][i], w["ffn1_w_up"][i], w["ffn1_w_down"][i],
                 row(w["ln1_g"][i]), row(w["ln1_b"][i]))
        q, k, v, c, g = _inproj(x, w["w_in"][i], cos, sin, s)
        shp = lambda a: a.reshape(b, s, -1)
        attn = _attention(shp(q), shp(k), shp(v), row(w["lam_q1"][i]), row(w["lam_k1"][i]),
                          row(w["lam_q2"][i]), row(w["lam_k2"][i]), row(w["subln_g"][i]), lam_init)
        x = _post(shp(c), attn, shp(g), shp(x), w["conv_dw"][i], row(w["conv_dw_b"][i]),
                  row(w["conv_ln_g"][i]), row(w["conv_ln_b"][i]), w["conv_pw2"][i], w["w_o"][i],
                  row(w["ln2_g"][i]), row(w["ln2_b"][i])).reshape(t, d)
        x = _ffn(x, w["ffn2_w_gate"][i], w["ffn2_w_up"][i], w["ffn2_w_down"][i],
                 row(w["ln3_g"][i]), row(w["ln3_b"][i]),
                 ple=(p[i].reshape(t, PLE_DIM), w["ple_w_gate"][i], w["ple_w_proj"][i]))
    return x.reshape(b, s, d)


_MATMUL_WEIGHTS = ("ffn1_w_gate", "ffn1_w_up", "ffn1_w_down", "w_in", "conv_pw2", "w_o",
                   "ffn2_w_gate", "ffn2_w_up", "ffn2_w_down", "ple_w_gate", "ple_w_proj")


def kernel(x_prompt, x_sample, p_prompt, p_sample, ffn1_w_gate, ffn1_w_up, ffn1_w_down, ln1_g, ln1_b, w_in, lam_q1, lam_k1, lam_q2, lam_k2, subln_g, conv_dw, conv_dw_b, conv_ln_g, conv_ln_b, conv_pw2, w_o, ln2_g, ln2_b, ffn2_w_gate, ffn2_w_up, ffn2_w_down, ple_w_gate, ple_w_proj, ln3_g, ln3_b):
    w = dict(ffn1_w_gate=ffn1_w_gate, ffn1_w_up=ffn1_w_up, ffn1_w_down=ffn1_w_down, ln1_g=ln1_g,
             ln1_b=ln1_b, w_in=w_in, lam_q1=lam_q1, lam_k1=lam_k1, lam_q2=lam_q2, lam_k2=lam_k2,
             subln_g=subln_g, conv_dw=conv_dw, conv_dw_b=conv_dw_b, conv_ln_g=conv_ln_g,
             conv_ln_b=conv_ln_b, conv_pw2=conv_pw2, w_o=w_o, ln2_g=ln2_g, ln2_b=ln2_b,
             ffn2_w_gate=ffn2_w_gate, ffn2_w_up=ffn2_w_up, ffn2_w_down=ffn2_w_down,
             ple_w_gate=ple_w_gate, ple_w_proj=ple_w_proj, ln3_g=ln3_g, ln3_b=ln3_b)
    for name in _MATMUL_WEIGHTS:
        w[name] = w[name].astype(BF16)
    return _trunk(x_prompt, p_prompt, w), _trunk(x_sample, p_sample, w)
```
